```python
import math
import jax, jax.numpy as jnp
from jax import lax
import numpy as np

D_MODEL = 1024
BATCH = 16
SEQ = 2048
DEPTH = 4

ATTN_WIDTH = D_MODEL // 2
RNN_WIDTH = D_MODEL - ATTN_WIDTH
DIFF_HEAD_DIM = 64
N_DIFF_HEADS = ATTN_WIDTH // (2 * DIFF_HEAD_DIM)
V_HEAD_DIM = 2 * DIFF_HEAD_DIM
ROT_DIM = DIFF_HEAD_DIM // 4
ROPE_THETA = 500000.0
LRU_BLOCK = 64
N_LRU_BLOCKS = RNN_WIDTH // LRU_BLOCK
CONV_WIDTH = 4
LRU_C = 8.0
D_FF = 4 * D_MODEL
PLE_DIM = 256
Q_BLOCK = 128
EPS = 1e-6
IN_WIDTH = 3 * ATTN_WIDTH + 2 * RNN_WIDTH

kernel_name = 'hymba_diffattn_rglru_hybrid'


def rms_norm(x, g):
    xf = x.astype(jnp.float32)
    y = xf * lax.rsqrt(jnp.mean(xf * xf, axis=-1, keepdims=True) + EPS)
    return (y * g.astype(jnp.float32)).astype(x.dtype)


def partial_rope(x, positions):
    half = ROT_DIM // 2
    inv_freq = ROPE_THETA ** (-jnp.arange(half, dtype=jnp.float32) * 2.0 / ROT_DIM)
    ang = positions.astype(jnp.float32)[..., None] * inv_freq
    cos = jnp.cos(ang)[:, :, None, None, :]
    sin = jnp.sin(ang)[:, :, None, None, :]
    xr = x[..., :ROT_DIM].astype(jnp.float32)
    x1, x2 = xr[..., :half], xr[..., half:]
    rot = jnp.concatenate([x1 * cos - x2 * sin, x2 * cos + x1 * sin], axis=-1)
    return jnp.concatenate([rot.astype(x.dtype), x[..., ROT_DIM:]], axis=-1)


def diff_attention(q, k, v, lam):
    B, S = q.shape[0], q.shape[1]
    nb = S // Q_BLOCK
    scale = DIFF_HEAD_DIM ** -0.5
    qb = q.reshape(B, nb, Q_BLOCK, N_DIFF_HEADS, 2, DIFF_HEAD_DIM).transpose(1, 0, 2, 3, 4, 5)
    q_idx = jnp.arange(S).reshape(nb, Q_BLOCK)
    k_idx = jnp.arange(S)

    def block(args):
        q_blk, qi = args
        s = jnp.einsum('bqhcd,bkhcd->bhcqk', q_blk, k).astype(jnp.float32) * scale
        mask = qi[:, None] >= k_idx[None, :]
        s = jnp.where(mask, s, -jnp.inf)
        pr = jax.nn.softmax(s, axis=-1)
        attn = pr[:, :, 0] - lam * pr[:, :, 1]
        return jnp.einsum('bhqk,bkhe->bqhe', attn.astype(v.dtype), v)

    out = lax.map(block, (qb, q_idx))
    return out.transpose(1, 0, 2, 3, 4).reshape(B, S, N_DIFF_HEADS, V_HEAD_DIM)


def causal_depthwise_conv(x, w, b):
    y = lax.conv_general_dilated(
        x, w[:, None, :].astype(x.dtype), window_strides=(1,),
        padding=[(CONV_WIDTH - 1, 0)],
        dimension_numbers=('NWC', 'WIO', 'NWC'),
        feature_group_count=x.shape[-1])
    return y + b


def rg_lru(x, w_a, b_a, w_x, b_x, lam_param):
    B, S, _ = x.shape
    xb = x.reshape(B, S, N_LRU_BLOCKS, LRU_BLOCK)
    r = jax.nn.sigmoid((jnp.einsum('bsgi,gij->bsgj', xb, w_a).reshape(B, S, RNN_WIDTH) + b_a).astype(jnp.float32))
    i = jax.nn.sigmoid((jnp.einsum('bsgi,gij->bsgj', xb, w_x).reshape(B, S, RNN_WIDTH) + b_x).astype(jnp.float32))
    log_a = -LRU_C * r * jax.nn.softplus(-lam_param.astype(jnp.float32))
    a = jnp.exp(log_a)
    b = jnp.sqrt(-jnp.expm1(2.0 * log_a)) * (i * x.astype(jnp.float32))

    def combine(left, right):
        a_l, b_l = left
        a_r, b_r = right
        return a_l * a_r, a_r * b_l + b_r

    _, h = lax.associative_scan(combine, (a, b), axis=1)
    return h.astype(x.dtype)


def setup_inputs(seed: int = 0) -> dict:
    key = jax.random.key(seed)
    ks = jax.random.split(key, 24)
    f32 = jnp.float32
    nrm = lambda k, shape, s: jax.random.normal(k, shape, f32) * s
    gain = lambda k, shape: 1.0 + 0.05 * jax.random.normal(k, shape, f32)
    a0 = jax.random.uniform(ks[15], (DEPTH, RNN_WIDTH), f32, 0.9, 0.999)
    s0 = a0 ** (1.0 / LRU_C)
    lru_lambda = jnp.log(s0) - jnp.log1p(-s0)
    positions = jnp.broadcast_to(jnp.arange(SEQ, dtype=jnp.int32)[None, :], (BATCH, SEQ))
    return {
        'x': nrm(ks[0], (BATCH, SEQ, D_MODEL), 1.0),
        'p': nrm(ks[1], (DEPTH, BATCH, SEQ, PLE_DIM), 1.0),
        'positions': positions,
        'w_in': nrm(ks[2], (DEPTH, D_MODEL, IN_WIDTH), D_MODEL ** -0.5),
        'w_out': nrm(ks[3], (DEPTH, D_MODEL, D_MODEL), D_MODEL ** -0.5),
        'g_mix': gain(ks[4], (DEPTH, D_MODEL)),
        'g_subln': gain(ks[5], (DEPTH, V_HEAD_DIM)),
        'lam_q': nrm(ks[6], (DEPTH, 2, DIFF_HEAD_DIM), 0.1),
        'lam_k': nrm(ks[7], (DEPTH, 2, DIFF_HEAD_DIM), 0.1),
        'conv_w': nrm(ks[8], (DEPTH, CONV_WIDTH, RNN_WIDTH), CONV_WIDTH ** -0.5),
        'conv_b': nrm(ks[9], (DEPTH, RNN_WIDTH), 0.01),
        'w_gate_a': nrm(ks[10], (DEPTH, N_LRU_BLOCKS, LRU_BLOCK, LRU_BLOCK), LRU_BLOCK ** -0.5),
        'b_gate_a': nrm(ks[11], (DEPTH, RNN_WIDTH), 0.01),
        'w_gate_x': nrm(ks[12], (DEPTH, N_LRU_BLOCKS, LRU_BLOCK, LRU_BLOCK), LRU_BLOCK ** -0.5),
        'b_gate_x': nrm(ks[13], (DEPTH, RNN_WIDTH), 0.01),
        'lru_lambda': lru_lambda,
        'g_mlp': gain(ks[14], (DEPTH, D_MODEL)),
        'w_mlp_in': nrm(ks[16], (DEPTH, D_MODEL, D_FF), D_MODEL ** -0.5),
        'w_mlp_out': nrm(ks[17], (DEPTH, D_FF, D_MODEL), D_FF ** -0.5),
        'g_ple': gain(ks[18], (DEPTH, D_MODEL)),
        'w_ple_gate': nrm(ks[19], (DEPTH, D_MODEL, D_MODEL), D_MODEL ** -0.5),
        'w_ple_proj': nrm(ks[20], (DEPTH, PLE_DIM, D_MODEL), PLE_DIM ** -0.5),
        'g_final': gain(ks[21], (D_MODEL,)),
    }


def reference(x, p, positions, w_in, w_out, g_mix, g_subln, lam_q, lam_k, conv_w, conv_b,
              w_gate_a, b_gate_a, w_gate_x, b_gate_x, lru_lambda, g_mlp, w_mlp_in, w_mlp_out,
              g_ple, w_ple_gate, w_ple_proj, g_final):
    B, S, _ = x.shape
    h = x
    A, R = ATTN_WIDTH, RNN_WIDTH
    for l in range(DEPTH):
        hn = rms_norm(h, g_mix[l])
        proj = hn @ w_in[l]
        q, k, v, xr, gr = jnp.split(proj, [A, 2 * A, 3 * A, 3 * A + R], axis=-1)
        q = partial_rope(q.reshape(B, S, N_DIFF_HEADS, 2, DIFF_HEAD_DIM), positions)
        k = partial_rope(k.reshape(B, S, N_DIFF_HEADS, 2, DIFF_HEAD_DIM), positions)
        v = v.reshape(B, S, N_DIFF_HEADS, V_HEAD_DIM)
        lam_init = 0.8 - 0.6 * math.exp(-0.3 * l)
        dots = jnp.sum(lam_q[l].astype(jnp.float32) * lam_k[l].astype(jnp.float32), axis=-1)
        lam = jnp.exp(dots[0]) - jnp.exp(dots[1]) + lam_init
        o = diff_attention(q, k, v, lam)
        o = (rms_norm(o, g_subln[l]) * (1.0 - lam_init)).reshape(B, S, A)
        xc = causal_depthwise_conv(xr, conv_w[l], conv_b[l])
        y = rg_lru(xc, w_gate_a[l], b_gate_a[l], w_gate_x[l], b_gate_x[l], lru_lambda[l])
        y = y * jax.nn.gelu(gr)
        h = h + jnp.concatenate([o, y], axis=-1) @ w_out[l]
        hm = rms_norm(h, g_mlp[l])
        h = h + jnp.square(jax.nn.relu(hm @ w_mlp_in[l])) @ w_mlp_out[l]
        gate = jax.nn.sigmoid(rms_norm(h, g_ple[l]) @ w_ple_gate[l])
        h = h + gate * (p[l] @ w_ple_proj[l])
    return rms_norm(h, g_final)
```

```python
import functools
import math

import jax
import jax.numpy as jnp
from jax import lax
from jax.experimental import pallas as pl
from jax.experimental.pallas import tpu as pltpu

F32 = jnp.float32
BF16 = jnp.bfloat16

D_MODEL = 1024
ATTN_WIDTH = 512
RNN_WIDTH = 512
HEAD_DIM = 64
N_HEADS = 4
V_DIM = 2 * HEAD_DIM
ROT_DIM = 16
ROT_HALF = ROT_DIM // 2
ROPE_THETA = 500000.0
LRU_BLOCK = 64
CONV_WIDTH = 4
LRU_C = 8.0
D_FF = 4 * D_MODEL
PLE_DIM = 256
EPS = 1e-6
IN_WIDTH = 3 * ATTN_WIDTH + 2 * RNN_WIDTH

LANES = 128
SUBLANES = 8
MXU_DIM = 256
VMEM_LIMIT_BYTES = 56 * 1024 * 1024

PROJ_ROWS = 512
RNN_ROWS = 512
POST_ROWS = 512
FF_CHUNK = 1024
ATTN_BLOCK = 256
ROPE_ROWS = 2048

LOG2E = 1.4426950408889634
NEG_BIG = -1e30


def _rms_norm(x, gain):
    ms = jnp.mean(x * x, axis=-1, keepdims=True)
    return x * lax.rsqrt(ms + EPS) * gain


def _sigmoid(x):
    return 1.0 / (1.0 + jnp.exp(-x))


def _gelu_tanh(x):
    inner = math.sqrt(2.0 / math.pi) * (x + 0.044715 * (x * x * x))
    return 0.5 * x * (1.0 + jnp.tanh(inner))


def _resident(shape):
    return pl.BlockSpec(shape, lambda *_: (0,) * len(shape), pipeline_mode=pl.Buffered(1))


def _rope_table_kernel(pos_ref, invf_ref, cos_ref, sin_ref):
    pos = pos_ref[...].astype(F32)
    ang = invf_ref[...] * pos
    cos16 = jnp.cos(ang)
    sin16 = jnp.sin(ang)
    row = lax.broadcasted_iota(jnp.int32, ang.shape, 0)
    sin16 = jnp.where(row < ROT_HALF, -sin16, sin16)
    ones = jnp.ones((HEAD_DIM - ROT_DIM, ang.shape[1]), F32)
    zeros = jnp.zeros_like(ones)
    cos_t = jnp.concatenate([cos16, ones, cos16, ones], axis=0)
    sin_t = jnp.concatenate([sin16, zeros, sin16, zeros], axis=0)
    cos_ref[...] = cos_t.T
    sin_ref[...] = sin_t.T


def _rope_tables(positions):
    tokens = positions.size
    pos = positions.reshape(1, tokens)
    d = jnp.arange(ROT_DIM) % ROT_HALF
    invf = (ROPE_THETA ** (-d.astype(F32) * 2.0 / ROT_DIM)).reshape(ROT_DIM, 1)
    grid = (tokens // ROPE_ROWS,)
    return pl.pallas_call(
        _rope_table_kernel,
        out_shape=(jax.ShapeDtypeStruct((tokens, LANES), F32),) * 2,
        grid=grid,
        in_specs=[pl.BlockSpec((1, ROPE_ROWS), lambda i: (0, i)),
                  pl.BlockSpec((ROT_DIM, 1), lambda i: (0, 0))],
        out_specs=(pl.BlockSpec((ROPE_ROWS, LANES), lambda i: (i, 0)),) * 2,
        compiler_params=pltpu.CompilerParams(dimension_semantics=("parallel",)),
        name="rope_tables",
    )(pos, invf)


def _in_proj_kernel(h_ref, g_ref, w_ref, cos_ref, sin_ref, q_ref, k_ref, v_ref, xr_ref, gr_ref):
    hn = _rms_norm(h_ref[...], g_ref[...]).astype(BF16)
    proj = jnp.dot(hn, w_ref[...], preferred_element_type=F32)
    cos = cos_ref[...]
    sin = sin_ref[...]
    lane = lax.broadcasted_iota(jnp.int32, cos.shape, 1) % HEAD_DIM
    first_half = lane < ROT_HALF

    def rope(x):
        partner = jnp.where(first_half,
                            pltpu.roll(x, LANES - ROT_HALF, 1),
                            pltpu.roll(x, ROT_HALF, 1))
        return x * cos + partner * sin

    q_scale = HEAD_DIM ** -0.5 * LOG2E
    for g in range(ATTN_WIDTH // LANES):
        cols = slice(g * LANES, (g + 1) * LANES)
        q_ref[:, cols] = (rope(proj[:, cols]) * q_scale).astype(BF16)
        kcols = slice(ATTN_WIDTH + g * LANES, ATTN_WIDTH + (g + 1) * LANES)
        k_ref[:, cols] = rope(proj[:, kcols]).astype(BF16)
    v_ref[...] = proj[:, 2 * ATTN_WIDTH:3 * ATTN_WIDTH].astype(BF16)
    xr_ref[...] = proj[:, 3 * ATTN_WIDTH:3 * ATTN_WIDTH + RNN_WIDTH]
    gr_ref[...] = proj[:, 3 * ATTN_WIDTH + RNN_WIDTH:]


def _in_proj(h, gain, w_in, cos_tab, sin_tab):
    tokens = h.shape[0]
    rows = PROJ_ROWS
    row_spec = lambda width: pl.BlockSpec((rows, width), lambda i: (i, 0))
    return pl.pallas_call(
        _in_proj_kernel,
        out_shape=(jax.ShapeDtypeStruct((tokens, ATTN_WIDTH), BF16),) * 3
        + (jax.ShapeDtypeStruct((tokens, RNN_WIDTH), F32),) * 2,
        grid=(tokens // rows,),
        in_specs=[row_spec(D_MODEL), _resident((1, D_MODEL)), _resident((D_MODEL, IN_WIDTH)),
                  row_spec(LANES), row_spec(LANES)],
        out_specs=(row_spec(ATTN_WIDTH),) * 3 + (row_spec(RNN_WIDTH),) * 2,
        compiler_params=pltpu.CompilerParams(dimension_semantics=("parallel",),
                                             vmem_limit_bytes=VMEM_LIMIT_BYTES),
        name="in_proj",
    )(h, gain, w_in, cos_tab, sin_tab)


def _rglru_kernel(xr_ref, gr_ref, cw_ref, cb_ref, wa_ref, ba_ref, wx_ref, bx_ref, lam_ref,
                  y_ref, xbuf_ref, state_ref):
    rows = xr_ref.shape[0]
    groups = rows // SUBLANES

    @pl.when(pl.program_id(1) == 0)
    def _():
        xbuf_ref[0:SUBLANES, :] = jnp.zeros((SUBLANES, RNN_WIDTH), F32)
        state_ref[...] = jnp.zeros_like(state_ref)

    xbuf_ref[SUBLANES:SUBLANES + rows, :] = xr_ref[...]
    xc = cb_ref[...]
    for tap in range(CONV_WIDTH):
        start = SUBLANES - (CONV_WIDTH - 1) + tap
        xc = xc + cw_ref[tap:tap + 1, :] * xbuf_ref[start:start + rows, :]
    xbuf_ref[0:SUBLANES, :] = xbuf_ref[rows:rows + SUBLANES, :]

    xcb = xc.astype(BF16)
    halves = RNN_WIDTH // MXU_DIM

    def gate(w_ref, b_ref):
        parts = [jnp.dot(xcb[:, j * MXU_DIM:(j + 1) * MXU_DIM], w_ref[j],
                         preferred_element_type=F32) for j in range(halves)]
        return _sigmoid(jnp.concatenate(parts, axis=-1) + b_ref[...])

    r = gate(wa_ref, ba_ref)
    i = gate(wx_ref, bx_ref)
    neg_lam = -lam_ref[...]
    softplus = jnp.maximum(neg_lam, 0.0) + jnp.log(1.0 + jnp.exp(-jnp.abs(neg_lam)))
    log_a = r * (-LRU_C * softplus)
    a = jnp.exp(log_a)
    b = jnp.sqrt(1.0 - a * a) * (i * xc)

    a3 = a.reshape(groups, SUBLANES, RNN_WIDTH)
    b3 = b.reshape(groups, SUBLANES, RNN_WIDTH)
    sub = lax.broadcasted_iota(jnp.int32, a3.shape, 1)
    shift = 1
    while shift < SUBLANES:
        keep = sub >= shift
        a_prev = jnp.where(keep, pltpu.roll(a3, shift, 1), 1.0)
        b_prev = jnp.where(keep, pltpu.roll(b3, shift, 1), 0.0)
        b3 = a3 * b_prev + b3
        a3 = a3 * a_prev
        shift *= 2
    h_prev = state_ref[0:1, :]
    out = []
    for g in range(groups):
        hg = a3[g] * h_prev + b3[g]
        out.append(hg)
        h_prev = hg[SUBLANES - 1:SUBLANES, :]
    state_ref[0:1, :] = h_prev
    hseq = jnp.concatenate(out, axis=0)
    y_ref[...] = (hseq * _gelu_tanh(gr_ref[...])).astype(BF16)


def _rglru(xr, gr, conv_w, conv_b, w_a, b_a, w_x, b_x, lam, batch):
    tokens = xr.shape[0]
    rows = RNN_ROWS
    steps = tokens // batch // rows
    row_spec = pl.BlockSpec((rows, RNN_WIDTH), lambda b, s: (b * steps + s, 0))
    halves = RNN_WIDTH // MXU_DIM
    return pl.pallas_call(
        _rglru_kernel,
        out_shape=jax.ShapeDtypeStruct((tokens, RNN_WIDTH), BF16),
        grid=(batch, steps),
        in_specs=[row_spec, row_spec,
                  _resident((CONV_WIDTH, RNN_WIDTH)), _resident((1, RNN_WIDTH)),
                  _resident((halves, MXU_DIM, MXU_DIM)), _resident((1, RNN_WIDTH)),
                  _resident((halves, MXU_DIM, MXU_DIM)), _resident((1, RNN_WIDTH)),
                  _resident((1, RNN_WIDTH))],
        out_specs=row_spec,
        scratch_shapes=[pltpu.VMEM((rows + SUBLANES, RNN_WIDTH), F32),
                        pltpu.VMEM((SUBLANES, RNN_WIDTH), F32)],
        compiler_params=pltpu.CompilerParams(dimension_semantics=("parallel", "arbitrary"),
                                             vmem_limit_bytes=VMEM_LIMIT_BYTES),
        name="rglru",
    )(xr, gr, conv_w, conv_b, w_a, b_a, w_x, b_x, lam)


def _diffattn_kernel(q_ref, k_ref, v_ref, lq_ref, lk_ref, g_ref, o_ref, m_ref, l_ref, acc_ref,
                     *, lam_init):
    blk = q_ref.shape[0]
    qi = pl.program_id(2)

    q = q_ref[...]
    lane = lax.broadcasted_iota(jnp.int32, q.shape, 1)
    zero = jnp.zeros_like(q)
    qs = jnp.concatenate([jnp.where(lane < HEAD_DIM, q, zero),
                          jnp.where(lane >= HEAD_DIM, q, zero)], axis=0)

    m_ref[...] = jnp.full_like(m_ref, NEG_BIG)
    l_ref[...] = jnp.zeros_like(l_ref)
    acc_ref[...] = jnp.zeros_like(acc_ref)

    def step(kj, masked):
        start = pl.multiple_of(kj * blk, blk)
        k = k_ref[pl.ds(start, blk), :]
        v = v_ref[pl.ds(start, blk), :]
        s = lax.dot_general(qs, k, (((1,), (1,)), ((), ())), preferred_element_type=F32)
        if masked:
            row = lax.broadcasted_iota(jnp.int32, s.shape, 0) % blk
            col = lax.broadcasted_iota(jnp.int32, s.shape, 1)
            s = jnp.where(col <= row, s, NEG_BIG)
        m_old = m_ref[...]
        m_new = jnp.maximum(m_old, jnp.max(s, axis=-1, keepdims=True))
        alpha = jnp.exp2(m_old - m_new)
        p = jnp.exp2(s - m_new)
        l_ref[...] = alpha * l_ref[...] + jnp.sum(p, axis=-1, keepdims=True)
        acc_ref[...] = alpha * acc_ref[...] + jnp.dot(p.astype(BF16), v,
                                                      preferred_element_type=F32)
        m_ref[...] = m_new

    def body(kj, carry):
        step(kj, masked=False)
        return carry

    lax.fori_loop(0, qi, body, 0)
    step(qi, masked=True)

    dots = jnp.sum(lq_ref[...] * lk_ref[...], axis=-1, keepdims=True)
    e = jnp.exp(dots)
    lam = e[0:1, :] - e[1:2, :] + lam_init
    out = acc_ref[...] / l_ref[...]
    o = out[:blk, :] - lam * out[blk:, :]
    o_ref[...] = (_rms_norm(o, g_ref[...]) * (1.0 - lam_init)).astype(BF16)


def _diffattn(q, k, v, lam_q, lam_k, g_subln, batch, lam_init):
    tokens = q.shape[0]
    seq = tokens // batch
    blk = ATTN_BLOCK
    nq = seq // blk
    q_spec = pl.BlockSpec((blk, V_DIM), lambda b, h, i: (b * nq + i, h))
    kv_spec = pl.BlockSpec((seq, V_DIM), lambda b, h, i: (b, h))
    return pl.pallas_call(
        functools.partial(_diffattn_kernel, lam_init=lam_init),
        out_shape=jax.ShapeDtypeStruct((tokens, ATTN_WIDTH), BF16),
        grid=(batch, N_HEADS, nq),
        in_specs=[q_spec, kv_spec, kv_spec,
                  pl.BlockSpec((2, HEAD_DIM), lambda b, h, i: (0, 0)),
                  pl.BlockSpec((2, HEAD_DIM), lambda b, h, i: (0, 0)),
                  pl.BlockSpec((1, V_DIM), lambda b, h, i: (0, 0))],
        out_specs=q_spec,
        scratch_shapes=[pltpu.VMEM((2 * blk, 1), F32), pltpu.VMEM((2 * blk, 1), F32),
                        pltpu.VMEM((2 * blk, V_DIM), F32)],
        compiler_params=pltpu.CompilerParams(
            dimension_semantics=("parallel", "parallel", "arbitrary")),
        name="diffattn",
    )(q, k, v, lam_q, lam_k, g_subln)


def _post_kernel(h_ref, o_ref, y_ref, p_ref, wout_ref, gmlp_ref, w1_ref, w2_ref, gple_ref,
                 wgate_ref, wple_ref, gfin_ref, out_ref, *, final):
    mix = jnp.concatenate([o_ref[...], y_ref[...]], axis=-1)
    h = h_ref[...] + jnp.dot(mix, wout_ref[...], preferred_element_type=F32)

    hm = _rms_norm(h, gmlp_ref[...]).astype(BF16)
    mlp = jnp.zeros_like(h)
    for c in range(D_FF // FF_CHUNK):
        ff = slice(c * FF_CHUNK, (c + 1) * FF_CHUNK)
        u = jnp.dot(hm, w1_ref[:, ff], preferred_element_type=F32)
        u = jnp.square(jnp.maximum(u, 0.0)).astype(BF16)
        mlp = mlp + jnp.dot(u, w2_ref[ff, :], preferred_element_type=F32)
    h = h + mlp

    hp = _rms_norm(h, gple_ref[...]).astype(BF16)
    gate = _sigmoid(jnp.dot(hp, wgate_ref[...], preferred_element_type=F32))
    emb = jnp.dot(p_ref[...].astype(BF16), wple_ref[...], preferred_element_type=F32)
    h = h + gate * emb
    if final:
        h = _rms_norm(h, gfin_ref[...])
    out_ref[...] = h


def _post(h, o, y, p_l, w_out, g_mlp, w1, w2, g_ple, w_gate, w_ple, g_final, final):
    tokens = h.shape[0]
    rows = POST_ROWS
    row_spec = lambda width: pl.BlockSpec((rows, width), lambda i: (i, 0))
    return pl.pallas_call(
        functools.partial(_post_kernel, final=final),
        out_shape=jax.ShapeDtypeStruct((tokens, D_MODEL), F32),
        grid=(tokens // rows,),
        in_specs=[row_spec(D_MODEL), row_spec(ATTN_WIDTH), row_spec(RNN_WIDTH), row_spec(PLE_DIM),
                  _resident((D_MODEL, D_MODEL)), _resident((1, D_MODEL)),
                  _resident((D_MODEL, D_FF)), _resident((D_FF, D_MODEL)),
                  _resident((1, D_MODEL)), _resident((D_MODEL, D_MODEL)),
                  _resident((PLE_DIM, D_MODEL)), _resident((1, D_MODEL))],
        out_specs=row_spec(D_MODEL),
        compiler_params=pltpu.CompilerParams(dimension_semantics=("parallel",),
                                             vmem_limit_bytes=VMEM_LIMIT_BYTES),
        name="post",
    )(h, o, y, p_l, w_out, g_mlp, w1, w2, g_ple, w_gate, w_ple, g_final)


def _block_diag_tiles(w):
    per_tile = MXU_DIM // LRU_BLOCK
    tiles = w.reshape(-1, per_tile, LRU_BLOCK, LRU_BLOCK)
    eye = jnp.eye(per_tile, dtype=w.dtype)
    dense = jnp.einsum('tgij,gh->tgihj', tiles, eye)
    return dense.reshape(-1, MXU_DIM, MXU_DIM)


def kernel(x, p, positions, w_in, w_out, g_mix, g_subln, lam_q, lam_k, conv_w, conv_b,
           w_gate_a, b_gate_a, w_gate_x, b_gate_x, lru_lambda, g_mlp, w_mlp_in, w_mlp_out,
           g_ple, w_ple_gate, w_ple_proj, g_final):
    batch, seq, _ = x.shape
    depth = w_in.shape[0]
    tokens = batch * seq
    h = x.reshape(tokens, D_MODEL)
    p = p.reshape(depth, tokens, PLE_DIM)
    cos_tab, sin_tab = _rope_tables(positions)
    row = lambda a: a.reshape(1, -1)
    for l in range(depth):
        lam_init = 0.8 - 0.6 * math.exp(-0.3 * l)
        q, k, v, xr, gr = _in_proj(h, row(g_mix[l]), w_in[l].astype(BF16), cos_tab, sin_tab)
        o = _diffattn(q, k, v, lam_q[l], lam_k[l], row(g_subln[l]), batch, lam_init)
        y = _rglru(xr, gr, conv_w[l], row(conv_b[l]),
                   _block_diag_tiles(w_gate_a[l]).astype(BF16), row(b_gate_a[l]),
                   _block_diag_tiles(w_gate_x[l]).astype(BF16), row(b_gate_x[l]),
                   row(lru_lambda[l]), batch)
        h = _post(h, o, y, p[l], w_out[l].astype(BF16), row(g_mlp[l]),
                  w_mlp_in[l].astype(BF16), w_mlp_out[l].astype(BF16), row(g_ple[l]),
                  w_ple_gate[l].astype(BF16), w_ple_proj[l].astype(BF16), row(g_final),
                  final=(l == depth - 1))
    return h.reshape(batch, seq, D_MODEL)
```

```python
import functools
import math

import jax
import jax.numpy as jnp
from jax import lax
from jax.experimental import pallas as pl
from jax.experimental.pallas import tpu as pltpu

F32 = jnp.float32
BF16 = jnp.bfloat16

D_MODEL = 1024
ATTN_WIDTH = 512
RNN_WIDTH = 512
HEAD_DIM = 64
N_HEADS = 4
V_DIM = 2 * HEAD_DIM
ROT_DIM = 16
ROT_HALF = ROT_DIM // 2
ROPE_THETA = 500000.0
LRU_BLOCK = 64
CONV_WIDTH = 4
LRU_C = 8.0
D_FF = 4 * D_MODEL
PLE_DIM = 256
EPS = 1e-6
IN_WIDTH = 3 * ATTN_WIDTH + 2 * RNN_WIDTH

LANES = 128
SUBLANES = 8
MXU_DIM = 256
VMEM_LIMIT_BYTES = 56 * 1024 * 1024

PROJ_ROWS = 512
RNN_ROWS = 512
POST_ROWS = 512
FF_CHUNK = 1024
ATTN_BLOCK = 256
ROPE_ROWS = 2048

LOG2E = 1.4426950408889634
NEG_BIG = -1e30


def _rms_norm(x, gain):
    ms = jnp.mean(x * x, axis=-1, keepdims=True)
    return x * lax.rsqrt(ms + EPS) * gain


def _sigmoid(x):
    return 1.0 / (1.0 + jnp.exp(-x))


def _gelu_tanh(x):
    inner = math.sqrt(2.0 / math.pi) * (x + 0.044715 * (x * x * x))
    return 0.5 * x * (1.0 + jnp.tanh(inner))


def _resident(shape):
    return pl.BlockSpec(shape, lambda *_: (0,) * len(shape), pipeline_mode=pl.Buffered(1))


def _rope_table_kernel(pos_ref, invf_ref, cos_ref, sin_ref):
    pos = pos_ref[...].astype(F32)
    ang = invf_ref[...] * pos
    cos16 = jnp.cos(ang)
    sin16 = jnp.sin(ang)
    row = lax.broadcasted_iota(jnp.int32, ang.shape, 0)
    sin16 = jnp.where(row < ROT_HALF, -sin16, sin16)
    ones = jnp.ones((HEAD_DIM - ROT_DIM, ang.shape[1]), F32)
    zeros = jnp.zeros_like(ones)
    cos_t = jnp.concatenate([cos16, ones, cos16, ones], axis=0)
    sin_t = jnp.concatenate([sin16, zeros, sin16, zeros], axis=0)
    cos_ref[...] = cos_t.T
    sin_ref[...] = sin_t.T


def _rope_tables(positions):
    tokens = positions.size
    pos = positions.reshape(1, tokens)
    d = jnp.arange(ROT_DIM) % ROT_HALF
    invf = (ROPE_THETA ** (-d.astype(F32) * 2.0 / ROT_DIM)).reshape(ROT_DIM, 1)
    grid = (tokens // ROPE_ROWS,)
    return pl.pallas_call(
        _rope_table_kernel,
        out_shape=(jax.ShapeDtypeStruct((tokens, LANES), F32),) * 2,
        grid=grid,
        in_specs=[pl.BlockSpec((1, ROPE_ROWS), lambda i: (0, i)),
                  pl.BlockSpec((ROT_DIM, 1), lambda i: (0, 0))],
        out_specs=(pl.BlockSpec((ROPE_ROWS, LANES), lambda i: (i, 0)),) * 2,
        compiler_params=pltpu.CompilerParams(dimension_semantics=("parallel",)),
        name="rope_tables",
    )(pos, invf)


def _in_proj_kernel(h_ref, g_ref, w_ref, cos_ref, sin_ref, qt_ref, k_ref, vt_ref, xr_ref, gr_ref):
    blk = qt_ref.shape[-1]
    hn = _rms_norm(h_ref[...], g_ref[...]).astype(BF16)
    proj = jnp.dot(hn, w_ref[...], preferred_element_type=F32)
    cos = cos_ref[...]
    sin = sin_ref[...]
    lane = lax.broadcasted_iota(jnp.int32, cos.shape, 1) % HEAD_DIM
    first_half = lane < ROT_HALF

    def rope(x):
        partner = jnp.where(first_half,
                            pltpu.roll(x, LANES - ROT_HALF, 1),
                            pltpu.roll(x, ROT_HALF, 1))
        return x * cos + partner * sin

    def store_transposed(dst_ref, head, x):
        xt = x.T.astype(BF16)
        for j in range(x.shape[0] // blk):
            dst_ref[head, j] = xt[:, j * blk:(j + 1) * blk]

    q_scale = HEAD_DIM ** -0.5 * LOG2E
    for head in range(N_HEADS):
        cols = slice(head * V_DIM, (head + 1) * V_DIM)
        kcols = slice(ATTN_WIDTH + head * V_DIM, ATTN_WIDTH + (head + 1) * V_DIM)
        vcols = slice(2 * ATTN_WIDTH + head * V_DIM, 2 * ATTN_WIDTH + (head + 1) * V_DIM)
        store_transposed(qt_ref, head, rope(proj[:, cols]) * q_scale)
        k_ref[:, cols] = rope(proj[:, kcols]).astype(BF16)
        store_transposed(vt_ref, head, proj[:, vcols])
    xr_ref[...] = proj[:, 3 * ATTN_WIDTH:3 * ATTN_WIDTH + RNN_WIDTH]
    gr_ref[...] = proj[:, 3 * ATTN_WIDTH + RNN_WIDTH:]


def _in_proj(h, gain, w_in, cos_tab, sin_tab, batch):
    tokens = h.shape[0]
    rows = PROJ_ROWS
    steps = tokens // batch // rows
    sub = rows // ATTN_BLOCK
    row_spec = lambda width: pl.BlockSpec((rows, width), lambda i: (i, 0))
    t_shape = jax.ShapeDtypeStruct((batch, N_HEADS, steps * sub, V_DIM, ATTN_BLOCK), BF16)
    t_spec = pl.BlockSpec((None, N_HEADS, sub, V_DIM, ATTN_BLOCK),
                          lambda i: (i // steps, 0, i % steps, 0, 0))
    return pl.pallas_call(
        _in_proj_kernel,
        out_shape=(t_shape, jax.ShapeDtypeStruct((tokens, ATTN_WIDTH), BF16), t_shape)
        + (jax.ShapeDtypeStruct((tokens, RNN_WIDTH), F32),) * 2,
        grid=(tokens // rows,),
        in_specs=[row_spec(D_MODEL), _resident((1, D_MODEL)), _resident((D_MODEL, IN_WIDTH)),
                  row_spec(LANES), row_spec(LANES)],
        out_specs=(t_spec, row_spec(ATTN_WIDTH), t_spec) + (row_spec(RNN_WIDTH),) * 2,
        compiler_params=pltpu.CompilerParams(dimension_semantics=("parallel",),
                                             vmem_limit_bytes=VMEM_LIMIT_BYTES),
        name="in_proj",
    )(h, gain, w_in, cos_tab, sin_tab)


def _rglru_kernel(xr_ref, gr_ref, cw_ref, cb_ref, wa_ref, ba_ref, wx_ref, bx_ref, lam_ref,
                  y_ref, xbuf_ref, state_ref):
    rows = xr_ref.shape[0]
    groups = rows // SUBLANES

    @pl.when(pl.program_id(1) == 0)
    def _():
        xbuf_ref[0:SUBLANES, :] = jnp.zeros((SUBLANES, RNN_WIDTH), F32)
        state_ref[...] = jnp.zeros_like(state_ref)

    xbuf_ref[SUBLANES:SUBLANES + rows, :] = xr_ref[...]
    xc = cb_ref[...]
    for tap in range(CONV_WIDTH):
        start = SUBLANES - (CONV_WIDTH - 1) + tap
        xc = xc + cw_ref[tap:tap + 1, :] * xbuf_ref[start:start + rows, :]
    xbuf_ref[0:SUBLANES, :] = xbuf_ref[rows:rows + SUBLANES, :]

    xcb = xc.astype(BF16)
    halves = RNN_WIDTH // MXU_DIM

    def gate(w_ref, b_ref):
        parts = [jnp.dot(xcb[:, j * MXU_DIM:(j + 1) * MXU_DIM], w_ref[j],
                         preferred_element_type=F32) for j in range(halves)]
        return _sigmoid(jnp.concatenate(parts, axis=-1) + b_ref[...])

    r = gate(wa_ref, ba_ref)
    i = gate(wx_ref, bx_ref)
    neg_lam = -lam_ref[...]
    softplus = jnp.maximum(neg_lam, 0.0) + jnp.log(1.0 + jnp.exp(-jnp.abs(neg_lam)))
    log_a = r * (-LRU_C * softplus)
    a = jnp.exp(log_a)
    b = jnp.sqrt(1.0 - a * a) * (i * xc)

    a3 = a.reshape(groups, SUBLANES, RNN_WIDTH)
    b3 = b.reshape(groups, SUBLANES, RNN_WIDTH)
    sub = lax.broadcasted_iota(jnp.int32, a3.shape, 1)
    shift = 1
    while shift < SUBLANES:
        keep = sub >= shift
        a_prev = jnp.where(keep, pltpu.roll(a3, shift, 1), 1.0)
        b_prev = jnp.where(keep, pltpu.roll(b3, shift, 1), 0.0)
        b3 = a3 * b_prev + b3
        a3 = a3 * a_prev
        shift *= 2
    h_prev = state_ref[0:1, :]
    out = []
    for g in range(groups):
        hg = a3[g] * h_prev + b3[g]
        out.append(hg)
        h_prev = hg[SUBLANES - 1:SUBLANES, :]
    state_ref[0:1, :] = h_prev
    hseq = jnp.concatenate(out, axis=0)
    y_ref[...] = (hseq * _gelu_tanh(gr_ref[...])).astype(BF16)


def _rglru(xr, gr, conv_w, conv_b, w_a, b_a, w_x, b_x, lam, batch):
    tokens = xr.shape[0]
    rows = RNN_ROWS
    steps = tokens // batch // rows
    row_spec = pl.BlockSpec((rows, RNN_WIDTH), lambda b, s: (b * steps + s, 0))
    halves = RNN_WIDTH // MXU_DIM
    return pl.pallas_call(
        _rglru_kernel,
        out_shape=jax.ShapeDtypeStruct((tokens, RNN_WIDTH), BF16),
        grid=(batch, steps),
        in_specs=[row_spec, row_spec,
                  _resident((CONV_WIDTH, RNN_WIDTH)), _resident((1, RNN_WIDTH)),
                  _resident((halves, MXU_DIM, MXU_DIM)), _resident((1, RNN_WIDTH)),
                  _resident((halves, MXU_DIM, MXU_DIM)), _resident((1, RNN_WIDTH)),
                  _resident((1, RNN_WIDTH))],
        out_specs=row_spec,
        scratch_shapes=[pltpu.VMEM((rows + SUBLANES, RNN_WIDTH), F32),
                        pltpu.VMEM((SUBLANES, RNN_WIDTH), F32)],
        compiler_params=pltpu.CompilerParams(dimension_semantics=("parallel", "arbitrary"),
                                             vmem_limit_bytes=VMEM_LIMIT_BYTES),
        name="rglru",
    )(xr, gr, conv_w, conv_b, w_a, b_a, w_x, b_x, lam)


def _diffattn_kernel(qt_ref, k_ref, vt_ref, lq_ref, lk_ref, g_ref, o_ref,
                     qs_ref, m_ref, l_ref, acc_ref, *, lam_init):
    blk = qt_ref.shape[-1]
    qi = pl.program_id(1)
    zeros_half = jnp.zeros((HEAD_DIM, blk), BF16)
    for h in range(N_HEADS):
        qt = qt_ref[h]
        q1 = jnp.concatenate([qt[:HEAD_DIM], zeros_half], axis=0)
        q2 = jnp.concatenate([zeros_half, qt[HEAD_DIM:]], axis=0)
        qs_ref[h] = jnp.concatenate([q1, q2], axis=1)

    def scores(h, kj):
        start = pl.multiple_of(kj * blk, blk)
        k = k_ref[pl.ds(start, blk), h * V_DIM:(h + 1) * V_DIM]
        return jnp.dot(k, qs_ref[h], preferred_element_type=F32)

    def first_update(h, kj, s):
        key = lax.broadcasted_iota(jnp.int32, s.shape, 0)
        qry = lax.broadcasted_iota(jnp.int32, s.shape, 1) % blk
        s = jnp.where(key <= qry, s, NEG_BIG)
        m = jnp.max(s, axis=0, keepdims=True)
        p = jnp.exp2(s - m)
        m_ref[h] = m
        l_ref[h] = jnp.sum(p, axis=0, keepdims=True)
        acc_ref[h] = jnp.dot(vt_ref[h, kj], p.astype(BF16), preferred_element_type=F32)

    def online_update(h, kj, s):
        m_old = m_ref[h]
        m_new = jnp.maximum(m_old, jnp.max(s, axis=0, keepdims=True))
        alpha = jnp.exp2(m_old - m_new)
        p = jnp.exp2(s - m_new)
        l_ref[h] = alpha * l_ref[h] + jnp.sum(p, axis=0, keepdims=True)
        acc_ref[h] = alpha * acc_ref[h] + jnp.dot(vt_ref[h, kj], p.astype(BF16),
                                                  preferred_element_type=F32)
        m_ref[h] = m_new

    def all_heads(kj, update):
        s_next = scores(0, kj)
        for h in range(N_HEADS):
            s = s_next
            if h + 1 < N_HEADS:
                s_next = scores(h + 1, kj)
            update(h, kj, s)

    all_heads(qi, first_update)

    def body(kj, carry):
        all_heads(kj, online_update)
        return carry

    lax.fori_loop(0, qi, body, 0)

    dots = jnp.sum(lq_ref[...] * lk_ref[...], axis=-1, keepdims=True)
    e = jnp.exp(dots)
    lam = e[0:1, :] - e[1:2, :] + lam_init
    for h in range(N_HEADS):
        out = acc_ref[h] / l_ref[h]
        o = (out[:, :blk] - lam * out[:, blk:]).T
        o_ref[:, h * V_DIM:(h + 1) * V_DIM] = (
            _rms_norm(o, g_ref[...]) * (1.0 - lam_init)).astype(BF16)


def _diffattn(qt, k, vt, lam_q, lam_k, g_subln, lam_init):
    batch, _, nblk, _, blk = qt.shape
    seq = nblk * blk
    return pl.pallas_call(
        functools.partial(_diffattn_kernel, lam_init=lam_init),
        out_shape=jax.ShapeDtypeStruct((batch * seq, ATTN_WIDTH), BF16),
        grid=(batch, nblk),
        in_specs=[pl.BlockSpec((None, N_HEADS, None, V_DIM, blk), lambda b, i: (b, 0, i, 0, 0)),
                  pl.BlockSpec((seq, ATTN_WIDTH), lambda b, i: (b, 0)),
                  pl.BlockSpec((None, N_HEADS, nblk, V_DIM, blk), lambda b, i: (b, 0, 0, 0, 0)),
                  pl.BlockSpec((2, HEAD_DIM), lambda b, i: (0, 0)),
                  pl.BlockSpec((2, HEAD_DIM), lambda b, i: (0, 0)),
                  pl.BlockSpec((1, V_DIM), lambda b, i: (0, 0))],
        out_specs=pl.BlockSpec((blk, ATTN_WIDTH), lambda b, i: (b * nblk + i, 0)),
        scratch_shapes=[pltpu.VMEM((N_HEADS, V_DIM, 2 * blk), BF16),
                        pltpu.VMEM((N_HEADS, 1, 2 * blk), F32),
                        pltpu.VMEM((N_HEADS, 1, 2 * blk), F32),
                        pltpu.VMEM((N_HEADS, V_DIM, 2 * blk), F32)],
        compiler_params=pltpu.CompilerParams(dimension_semantics=("parallel", "arbitrary"),
                                             vmem_limit_bytes=VMEM_LIMIT_BYTES),
        name="diffattn",
    )(qt, k, vt, lam_q, lam_k, g_subln)


def _post_kernel(h_ref, o_ref, y_ref, p_ref, wout_ref, gmlp_ref, w1_ref, w2_ref, gple_ref,
                 wgate_ref, wple_ref, gfin_ref, out_ref, *, final):
    mix = jnp.concatenate([o_ref[...], y_ref[...]], axis=-1)
    h = h_ref[...] + jnp.dot(mix, wout_ref[...], preferred_element_type=F32)

    hm = _rms_norm(h, gmlp_ref[...]).astype(BF16)
    mlp = jnp.zeros_like(h)
    for c in range(D_FF // FF_CHUNK):
        ff = slice(c * FF_CHUNK, (c + 1) * FF_CHUNK)
        u = jnp.dot(hm, w1_ref[:, ff], preferred_element_type=F32)
        u = jnp.square(jnp.maximum(u, 0.0)).astype(BF16)
        mlp = mlp + jnp.dot(u, w2_ref[ff, :], preferred_element_type=F32)
    h = h + mlp

    hp = _rms_norm(h, gple_ref[...]).astype(BF16)
    gate = _sigmoid(jnp.dot(hp, wgate_ref[...], preferred_element_type=F32))
    emb = jnp.dot(p_ref[...].astype(BF16), wple_ref[...], preferred_element_type=F32)
    h = h + gate * emb
    if final:
        h = _rms_norm(h, gfin_ref[...])
    out_ref[...] = h


def _post(h, o, y, p_l, w_out, g_mlp, w1, w2, g_ple, w_gate, w_ple, g_final, final):
    tokens = h.shape[0]
    rows = POST_ROWS
    row_spec = lambda width: pl.BlockSpec((rows, width), lambda i: (i, 0))
    return pl.pallas_call(
        functools.partial(_post_kernel, final=final),
        out_shape=jax.ShapeDtypeStruct((tokens, D_MODEL), F32),
        grid=(tokens // rows,),
        in_specs=[row_spec(D_MODEL), row_spec(ATTN_WIDTH), row_spec(RNN_WIDTH), row_spec(PLE_DIM),
                  _resident((D_MODEL, D_MODEL)), _resident((1, D_MODEL)),
                  _resident((D_MODEL, D_FF)), _resident((D_FF, D_MODEL)),
                  _resident((1, D_MODEL)), _resident((D_MODEL, D_MODEL)),
                  _resident((PLE_DIM, D_MODEL)), _resident((1, D_MODEL))],
        out_specs=row_spec(D_MODEL),
        compiler_params=pltpu.CompilerParams(dimension_semantics=("parallel",),
                                             vmem_limit_bytes=VMEM_LIMIT_BYTES),
        name="post",
    )(h, o, y, p_l, w_out, g_mlp, w1, w2, g_ple, w_gate, w_ple, g_final)


def _block_diag_tiles(w):
    per_tile = MXU_DIM // LRU_BLOCK
    tiles = w.reshape(-1, per_tile, LRU_BLOCK, LRU_BLOCK)
    eye = jnp.eye(per_tile, dtype=w.dtype)
    dense = jnp.einsum('tgij,gh->tgihj', tiles, eye)
    return dense.reshape(-1, MXU_DIM, MXU_DIM)


def kernel(x, p, positions, w_in, w_out, g_mix, g_subln, lam_q, lam_k, conv_w, conv_b,
           w_gate_a, b_gate_a, w_gate_x, b_gate_x, lru_lambda, g_mlp, w_mlp_in, w_mlp_out,
           g_ple, w_ple_gate, w_ple_proj, g_final):
    batch, seq, _ = x.shape
    depth = w_in.shape[0]
    tokens = batch * seq
    h = x.reshape(tokens, D_MODEL)
    p = p.reshape(depth, tokens, PLE_DIM)
    cos_tab, sin_tab = _rope_tables(positions)
    row = lambda a: a.reshape(1, -1)
    for l in range(depth):
        lam_init = 0.8 - 0.6 * math.exp(-0.3 * l)
        qt, k, vt, xr, gr = _in_proj(h, row(g_mix[l]), w_in[l].astype(BF16), cos_tab, sin_tab,
                                     batch)
        o = _diffattn(qt, k, vt, lam_q[l], lam_k[l], row(g_subln[l]), lam_init)
        y = _rglru(xr, gr, conv_w[l], row(conv_b[l]),
                   _block_diag_tiles(w_gate_a[l]).astype(BF16), row(b_gate_a[l]),
                   _block_diag_tiles(w_gate_x[l]).astype(BF16), row(b_gate_x[l]),
                   row(lru_lambda[l]), batch)
        h = _post(h, o, y, p[l], w_out[l].astype(BF16), row(g_mlp[l]),
                  w_mlp_in[l].astype(BF16), w_mlp_out[l].astype(BF16), row(g_ple[l]),
                  w_ple_gate[l].astype(BF16), w_ple_proj[l].astype(BF16), row(g_final),
                  final=(l == depth - 1))
    return h.reshape(batch, seq, D_MODEL)
```

```python
import functools
import math

import jax
import jax.numpy as jnp
from jax import lax
from jax.experimental import pallas as pl
from jax.experimental.pallas import tpu as pltpu

F32 = jnp.float32
BF16 = jnp.bfloat16

D_MODEL = 1024
ATTN_WIDTH = 512
RNN_WIDTH = 512
HEAD_DIM = 64
N_HEADS = 4
V_DIM = 2 * HEAD_DIM
ROT_DIM = 16
ROT_HALF = ROT_DIM // 2
ROPE_THETA = 500000.0
LRU_BLOCK = 64
CONV_WIDTH = 4
LRU_C = 8.0
D_FF = 4 * D_MODEL
PLE_DIM = 256
EPS = 1e-6
IN_WIDTH = 3 * ATTN_WIDTH + 2 * RNN_WIDTH

LANES = 128
SUBLANES = 8
MXU_DIM = 256
VMEM_LIMIT_BYTES = 56 * 1024 * 1024

PROJ_ROWS = 512
RNN_ROWS = 512
POST_ROWS = 512
FF_CHUNK = 1024
ATTN_BLOCK = 512
ROPE_ROWS = 2048

LOG2E = 1.4426950408889634
NEG_BIG = -1e30


def _rms_norm(x, gain):
    ms = jnp.mean(x * x, axis=-1, keepdims=True)
    return x * lax.rsqrt(ms + EPS) * gain


def _sigmoid(x):
    return 1.0 / (1.0 + jnp.exp(-x))


def _gelu_tanh(x):
    inner = math.sqrt(2.0 / math.pi) * (x + 0.044715 * (x * x * x))
    return 0.5 * x * (1.0 + jnp.tanh(inner))


def _resident(shape):
    return pl.BlockSpec(shape, lambda *_: (0,) * len(shape), pipeline_mode=pl.Buffered(1))


def _rope_table_kernel(pos_ref, invf_ref, cos_ref, sin_ref):
    pos = pos_ref[...].astype(F32)
    ang = invf_ref[...] * pos
    cos16 = jnp.cos(ang)
    sin16 = jnp.sin(ang)
    row = lax.broadcasted_iota(jnp.int32, ang.shape, 0)
    sin16 = jnp.where(row < ROT_HALF, -sin16, sin16)
    ones = jnp.ones((HEAD_DIM - ROT_DIM, ang.shape[1]), F32)
    zeros = jnp.zeros_like(ones)
    cos_t = jnp.concatenate([cos16, ones, cos16, ones], axis=0)
    sin_t = jnp.concatenate([sin16, zeros, sin16, zeros], axis=0)
    cos_ref[...] = cos_t.T
    sin_ref[...] = sin_t.T


def _rope_tables(positions):
    tokens = positions.size
    pos = positions.reshape(1, tokens)
    d = jnp.arange(ROT_DIM) % ROT_HALF
    invf = (ROPE_THETA ** (-d.astype(F32) * 2.0 / ROT_DIM)).reshape(ROT_DIM, 1)
    grid = (tokens // ROPE_ROWS,)
    return pl.pallas_call(
        _rope_table_kernel,
        out_shape=(jax.ShapeDtypeStruct((tokens, LANES), F32),) * 2,
        grid=grid,
        in_specs=[pl.BlockSpec((1, ROPE_ROWS), lambda i: (0, i)),
                  pl.BlockSpec((ROT_DIM, 1), lambda i: (0, 0))],
        out_specs=(pl.BlockSpec((ROPE_ROWS, LANES), lambda i: (i, 0)),) * 2,
        compiler_params=pltpu.CompilerParams(dimension_semantics=("parallel",)),
        name="rope_tables",
    )(pos, invf)


def _in_proj_kernel(h_ref, g_ref, w_ref, cos_ref, sin_ref, qt_ref, k_ref, vt_ref, xr_ref, gr_ref):
    blk = qt_ref.shape[-1]
    hn = _rms_norm(h_ref[...], g_ref[...]).astype(BF16)
    proj = jnp.dot(hn, w_ref[...], preferred_element_type=F32)
    cos = cos_ref[...]
    sin = sin_ref[...]
    lane = lax.broadcasted_iota(jnp.int32, cos.shape, 1) % HEAD_DIM
    first_half = lane < ROT_HALF

    def rope(x):
        partner = jnp.where(first_half,
                            pltpu.roll(x, LANES - ROT_HALF, 1),
                            pltpu.roll(x, ROT_HALF, 1))
        return x * cos + partner * sin

    def store_transposed(dst_ref, head, x):
        xt = x.T.astype(BF16)
        for j in range(x.shape[0] // blk):
            dst_ref[head, j] = xt[:, j * blk:(j + 1) * blk]

    q_scale = HEAD_DIM ** -0.5 * LOG2E
    for head in range(N_HEADS):
        cols = slice(head * V_DIM, (head + 1) * V_DIM)
        kcols = slice(ATTN_WIDTH + head * V_DIM, ATTN_WIDTH + (head + 1) * V_DIM)
        vcols = slice(2 * ATTN_WIDTH + head * V_DIM, 2 * ATTN_WIDTH + (head + 1) * V_DIM)
        store_transposed(qt_ref, head, rope(proj[:, cols]) * q_scale)
        k_ref[:, cols] = rope(proj[:, kcols]).astype(BF16)
        store_transposed(vt_ref, head, proj[:, vcols])
    rows = proj.shape[0]
    seg = rows // SUBLANES
    for c in range(RNN_WIDTH // LANES):
        xcol = 3 * ATTN_WIDTH + c * LANES
        for t in range(0, rows, SUBLANES):
            dst = pl.ds((t % seg) * SUBLANES + t // seg, SUBLANES, stride=SUBLANES)
            xr_ref[c, dst, :] = proj[t:t + SUBLANES, xcol:xcol + LANES]
            gr_ref[c, dst, :] = proj[t:t + SUBLANES, xcol + RNN_WIDTH:xcol + RNN_WIDTH + LANES]


def _in_proj(h, gain, w_in, cos_tab, sin_tab, batch):
    tokens = h.shape[0]
    rows = PROJ_ROWS
    steps = tokens // batch // rows
    sub = rows // ATTN_BLOCK
    row_spec = lambda width: pl.BlockSpec((rows, width), lambda i: (i, 0))
    t_shape = jax.ShapeDtypeStruct((batch, N_HEADS, steps * sub, V_DIM, ATTN_BLOCK), BF16)
    t_spec = pl.BlockSpec((None, N_HEADS, sub, V_DIM, ATTN_BLOCK),
                          lambda i: (i // steps, 0, i % steps, 0, 0))
    groups = RNN_WIDTH // LANES
    g_spec = pl.BlockSpec((groups, rows, LANES), lambda i: (0, i, 0))
    return pl.pallas_call(
        _in_proj_kernel,
        out_shape=(t_shape, jax.ShapeDtypeStruct((tokens, ATTN_WIDTH), BF16), t_shape)
        + (jax.ShapeDtypeStruct((groups, tokens, LANES), F32),) * 2,
        grid=(tokens // rows,),
        in_specs=[row_spec(D_MODEL), _resident((1, D_MODEL)), _resident((D_MODEL, IN_WIDTH)),
                  row_spec(LANES), row_spec(LANES)],
        out_specs=(t_spec, row_spec(ATTN_WIDTH), t_spec, g_spec, g_spec),
        compiler_params=pltpu.CompilerParams(dimension_semantics=("parallel",),
                                             vmem_limit_bytes=VMEM_LIMIT_BYTES),
        name="in_proj",
    )(h, gain, w_in, cos_tab, sin_tab)


def _rglru_kernel(xr_ref, gr_ref, cw_ref, cb_ref, wa_ref, ba_ref, wx_ref, bx_ref, lam_ref,
                  y_ref, halo_ref, state_ref, ybuf_ref):
    lane_groups, rows, _ = xr_ref.shape
    seg = rows // SUBLANES
    taps_before = CONV_WIDTH - 1

    @pl.when(pl.program_id(1) == 0)
    def _():
        halo_ref[...] = jnp.zeros_like(halo_ref)
        state_ref[...] = jnp.zeros_like(state_ref)

    def interleaved(ref):
        return jnp.concatenate([ref[c].reshape(seg, SUBLANES, LANES)
                                for c in range(lane_groups)], axis=-1)

    x = interleaved(xr_ref)
    gr = interleaved(gr_ref)
    sub = lax.broadcasted_iota(jnp.int32, (SUBLANES, RNN_WIDTH), 0)

    before = [jnp.where(sub == 0, pltpu.roll(halo_ref[k], 1, 0),
                        pltpu.roll(x[seg - taps_before + k], 1, 0)) for k in range(taps_before)]
    xe = jnp.concatenate([jnp.stack(before, axis=0), x], axis=0)
    halo_ref[...] = x[seg - taps_before:]
    xc = cb_ref[...] + cw_ref[0:1, :] * xe[0:seg]
    for k in range(1, CONV_WIDTH):
        xc = xc + cw_ref[k:k + 1, :] * xe[k:k + seg]

    xcb = xc.reshape(rows, RNN_WIDTH).astype(BF16)

    def gate_tanh(w_ref, b_ref):
        parts = [jnp.dot(xcb[:, j * MXU_DIM:(j + 1) * MXU_DIM], w_ref[j],
                         preferred_element_type=F32) for j in range(RNN_WIDTH // MXU_DIM)]
        pre = jnp.concatenate(parts, axis=-1) + b_ref[...]
        return jnp.tanh(0.5 * pre).reshape(seg, SUBLANES, RNN_WIDTH)

    tr = gate_tanh(wa_ref, ba_ref)
    ti = gate_tanh(wx_ref, bx_ref)
    neg_lam = -lam_ref[...]
    softplus = jnp.maximum(neg_lam, 0.0) + jnp.log(1.0 + jnp.exp(-jnp.abs(neg_lam)))
    a = jnp.exp2((tr + 1.0) * ((-0.5 * LRU_C * LOG2E) * softplus))
    z = 1.0 - a * a
    root = jnp.where(z > 0.0, z * lax.rsqrt(z), 0.0)
    b = root * ((ti + 1.0) * (0.5 * xc))

    h_end = jnp.zeros((SUBLANES, RNN_WIDTH), F32)
    decay = jnp.ones((SUBLANES, RNN_WIDTH), F32)
    for g in range(seg):
        h_end = a[g] * h_end + b[g]
        decay = a[g] * decay
    carry = state_ref[0:1, :]
    carries = []
    for s in range(SUBLANES):
        carries.append(carry)
        carry = decay[s:s + 1, :] * carry + h_end[s:s + 1, :]
    state_ref[0:1, :] = carry
    h = jnp.concatenate(carries, axis=0)
    for g in range(seg):
        h = a[g] * h + b[g]
        y = h * _gelu_tanh(gr[g])
        for c in range(lane_groups):
            ybuf_ref[c, pl.ds(g, SUBLANES, stride=seg), :] = y[:, c * LANES:(c + 1) * LANES]
    for c in range(lane_groups):
        y_ref[:, c * LANES:(c + 1) * LANES] = ybuf_ref[c].astype(BF16)


def _rglru(xr, gr, conv_w, conv_b, w_a, b_a, w_x, b_x, lam, batch):
    lane_groups, tokens, _ = xr.shape
    rows = RNN_ROWS
    steps = tokens // batch // rows
    g_spec = pl.BlockSpec((lane_groups, rows, LANES), lambda b, s: (0, b * steps + s, 0))
    row_spec = pl.BlockSpec((rows, RNN_WIDTH), lambda b, s: (b * steps + s, 0))
    halves = RNN_WIDTH // MXU_DIM
    return pl.pallas_call(
        _rglru_kernel,
        out_shape=jax.ShapeDtypeStruct((tokens, RNN_WIDTH), BF16),
        grid=(batch, steps),
        in_specs=[g_spec, g_spec,
                  _resident((CONV_WIDTH, RNN_WIDTH)), _resident((1, RNN_WIDTH)),
                  _resident((halves, MXU_DIM, MXU_DIM)), _resident((1, RNN_WIDTH)),
                  _resident((halves, MXU_DIM, MXU_DIM)), _resident((1, RNN_WIDTH)),
                  _resident((1, RNN_WIDTH))],
        out_specs=row_spec,
        scratch_shapes=[pltpu.VMEM((CONV_WIDTH - 1, SUBLANES, RNN_WIDTH), F32),
                        pltpu.VMEM((SUBLANES, RNN_WIDTH), F32),
                        pltpu.VMEM((lane_groups, rows, LANES), F32)],
        compiler_params=pltpu.CompilerParams(dimension_semantics=("parallel", "arbitrary"),
                                             vmem_limit_bytes=VMEM_LIMIT_BYTES),
        name="rglru",
    )(xr, gr, conv_w, conv_b, w_a, b_a, w_x, b_x, lam)


def _diffattn_kernel(qt_ref, k_ref, vt_ref, lq_ref, lk_ref, g_ref, o_ref,
                     qs_ref, s0_ref, m_ref, l_ref, acc_ref, *, lam_init):
    blk = qt_ref.shape[-1]
    qi = pl.program_id(1)
    zeros_half = jnp.zeros((HEAD_DIM, blk), BF16)
    for h in range(N_HEADS):
        qt = qt_ref[h]
        q1 = jnp.concatenate([qt[:HEAD_DIM], zeros_half], axis=0)
        q2 = jnp.concatenate([zeros_half, qt[HEAD_DIM:]], axis=0)
        qs_ref[h] = jnp.concatenate([q1, q2], axis=1)

    def scores(h, kj):
        start = pl.multiple_of(kj * blk, blk)
        k = k_ref[pl.ds(start, blk), h * V_DIM:(h + 1) * V_DIM]
        return jnp.dot(k, qs_ref[h], preferred_element_type=F32)

    m_ref[...] = jnp.full_like(m_ref, NEG_BIG)
    l_ref[...] = jnp.zeros_like(l_ref)
    acc_ref[...] = jnp.zeros_like(acc_ref)

    def update(h, kj, s, diagonal):
        if diagonal:
            key = lax.broadcasted_iota(jnp.int32, s.shape, 0)
            qry = lax.broadcasted_iota(jnp.int32, s.shape, 1) % blk
            s = jnp.where(key <= qry, s, NEG_BIG)
        m_old = m_ref[h]
        m_new = jnp.maximum(m_old, jnp.max(s, axis=0, keepdims=True))
        alpha = jnp.exp2(m_old - m_new)
        p = jnp.exp2(s - m_new)
        l_ref[h] = alpha * l_ref[h] + jnp.sum(p, axis=0, keepdims=True)
        acc_ref[h] = alpha * acc_ref[h] + jnp.dot(vt_ref[h, kj], p.astype(BF16),
                                                  preferred_element_type=F32)
        m_ref[h] = m_new

    def all_heads(kj, diagonal):
        s_next = s0_ref[...]
        for h in range(N_HEADS):
            s = s_next
            if h + 1 < N_HEADS:
                s_next = scores(h + 1, kj)
            elif not diagonal:
                s0_ref[...] = scores(0, kj + 1)
            update(h, kj, s, diagonal)

    s0_ref[...] = scores(0, 0)

    def body(kj, carry):
        all_heads(kj, diagonal=False)
        return carry

    lax.fori_loop(0, qi, body, 0)
    all_heads(qi, diagonal=True)

    dots = jnp.sum(lq_ref[...] * lk_ref[...], axis=-1, keepdims=True)
    e = jnp.exp(dots)
    lam = e[0:1, :] - e[1:2, :] + lam_init
    for h in range(N_HEADS):
        out = acc_ref[h] / l_ref[h]
        o = (out[:, :blk] - lam * out[:, blk:]).T
        o_ref[:, h * V_DIM:(h + 1) * V_DIM] = (
            _rms_norm(o, g_ref[...]) * (1.0 - lam_init)).astype(BF16)


def _diffattn(qt, k, vt, lam_q, lam_k, g_subln, lam_init):
    batch, _, nblk, _, blk = qt.shape
    seq = nblk * blk
    return pl.pallas_call(
        functools.partial(_diffattn_kernel, lam_init=lam_init),
        out_shape=jax.ShapeDtypeStruct((batch * seq, ATTN_WIDTH), BF16),
        grid=(batch, nblk),
        in_specs=[pl.BlockSpec((None, N_HEADS, None, V_DIM, blk), lambda b, i: (b, 0, i, 0, 0)),
                  pl.BlockSpec((seq, ATTN_WIDTH), lambda b, i: (b, 0)),
                  pl.BlockSpec((None, N_HEADS, nblk, V_DIM, blk), lambda b, i: (b, 0, 0, 0, 0)),
                  pl.BlockSpec((2, HEAD_DIM), lambda b, i: (0, 0)),
                  pl.BlockSpec((2, HEAD_DIM), lambda b, i: (0, 0)),
                  pl.BlockSpec((1, V_DIM), lambda b, i: (0, 0))],
        out_specs=pl.BlockSpec((blk, ATTN_WIDTH), lambda b, i: (b * nblk + i, 0)),
        scratch_shapes=[pltpu.VMEM((N_HEADS, V_DIM, 2 * blk), BF16),
                        pltpu.VMEM((blk, 2 * blk), F32),
                        pltpu.VMEM((N_HEADS, 1, 2 * blk), F32),
                        pltpu.VMEM((N_HEADS, 1, 2 * blk), F32),
                        pltpu.VMEM((N_HEADS, V_DIM, 2 * blk), F32)],
        compiler_params=pltpu.CompilerParams(dimension_semantics=("parallel", "arbitrary"),
                                             vmem_limit_bytes=VMEM_LIMIT_BYTES),
        name="diffattn",
    )(qt, k, vt, lam_q, lam_k, g_subln)


def _post_kernel(h_ref, o_ref, y_ref, p_ref, wout_ref, gmlp_ref, w1_ref, w2_ref, gple_ref,
                 wgate_ref, wple_ref, gfin_ref, out_ref, *, final):
    mix = jnp.concatenate([o_ref[...], y_ref[...]], axis=-1)
    h = h_ref[...] + jnp.dot(mix, wout_ref[...], preferred_element_type=F32)

    hm = _rms_norm(h, gmlp_ref[...]).astype(BF16)
    mlp = jnp.zeros_like(h)
    for c in range(D_FF // FF_CHUNK):
        ff = slice(c * FF_CHUNK, (c + 1) * FF_CHUNK)
        u = jnp.dot(hm, w1_ref[:, ff], preferred_element_type=F32)
        u = jnp.square(jnp.maximum(u, 0.0)).astype(BF16)
        mlp = mlp + jnp.dot(u, w2_ref[ff, :], preferred_element_type=F32)
    h = h + mlp

    hp = _rms_norm(h, gple_ref[...]).astype(BF16)
    gate = _sigmoid(jnp.dot(hp, wgate_ref[...], preferred_element_type=F32))
    emb = jnp.dot(p_ref[...].astype(BF16), wple_ref[...], preferred_element_type=F32)
    h = h + gate * emb
    if final:
        h = _rms_norm(h, gfin_ref[...])
    out_ref[...] = h


def _post(h, o, y, p_l, w_out, g_mlp, w1, w2, g_ple, w_gate, w_ple, g_final, final):
    tokens = h.shape[0]
    rows = POST_ROWS
    row_spec = lambda width: pl.BlockSpec((rows, width), lambda i: (i, 0))
    return pl.pallas_call(
        functools.partial(_post_kernel, final=final),
        out_shape=jax.ShapeDtypeStruct((tokens, D_MODEL), F32),
        grid=(tokens // rows,),
        in_specs=[row_spec(D_MODEL), row_spec(ATTN_WIDTH), row_spec(RNN_WIDTH), row_spec(PLE_DIM),
                  _resident((D_MODEL, D_MODEL)), _resident((1, D_MODEL)),
                  _resident((D_MODEL, D_FF)), _resident((D_FF, D_MODEL)),
                  _resident((1, D_MODEL)), _resident((D_MODEL, D_MODEL)),
                  _resident((PLE_DIM, D_MODEL)), _resident((1, D_MODEL))],
        out_specs=row_spec(D_MODEL),
        compiler_params=pltpu.CompilerParams(dimension_semantics=("parallel",),
                                             vmem_limit_bytes=VMEM_LIMIT_BYTES),
        name="post",
    )(h, o, y, p_l, w_out, g_mlp, w1, w2, g_ple, w_gate, w_ple, g_final)


def _block_diag_tiles(w):
    per_tile = MXU_DIM // LRU_BLOCK
    tiles = w.reshape(-1, per_tile, LRU_BLOCK, LRU_BLOCK)
    eye = jnp.eye(per_tile, dtype=w.dtype)
    dense = jnp.einsum('tgij,gh->tgihj', tiles, eye)
    return dense.reshape(-1, MXU_DIM, MXU_DIM)


def kernel(x, p, positions, w_in, w_out, g_mix, g_subln, lam_q, lam_k, conv_w, conv_b,
           w_gate_a, b_gate_a, w_gate_x, b_gate_x, lru_lambda, g_mlp, w_mlp_in, w_mlp_out,
           g_ple, w_ple_gate, w_ple_proj, g_final):
    batch, seq, _ = x.shape
    depth = w_in.shape[0]
    tokens = batch * seq
    h = x.reshape(tokens, D_MODEL)
    p = p.reshape(depth, tokens, PLE_DIM)
    cos_tab, sin_tab = _rope_tables(positions)
    row = lambda a: a.reshape(1, -1)
    for l in range(depth):
        lam_init = 0.8 - 0.6 * math.exp(-0.3 * l)
        qt, k, vt, xr, gr = _in_proj(h, row(g_mix[l]), w_in[l].astype(BF16), cos_tab, sin_tab,
                                     batch)
        o = _diffattn(qt, k, vt, lam_q[l], lam_k[l], row(g_subln[l]), lam_init)
        y = _rglru(xr, gr, conv_w[l], row(conv_b[l]),
                   _block_diag_tiles(w_gate_a[l]).astype(BF16), row(b_gate_a[l]),
                   _block_diag_tiles(w_gate_x[l]).astype(BF16), row(b_gate_x[l]),
                   row(lru_lambda[l]), batch)
        h = _post(h, o, y, p[l], w_out[l].astype(BF16), row(g_mlp[l]),
                  w_mlp_in[l].astype(BF16), w_mlp_out[l].astype(BF16), row(g_ple[l]),
                  w_ple_gate[l].astype(BF16), w_ple_proj[l].astype(BF16), row(g_final),
                  final=(l == depth - 1))
    return h.reshape(batch, seq, D_MODEL)
```
